```python
import jax, jax.numpy as jnp
from jax import lax
import numpy as np

D_MODEL = 1024
BATCH = 2
SEQ = 8192
DEPTH = 2
DEC_BATCH = 32
DEC_SEQ = 2048
PAST_LEN = 128

N_HEADS = 8
QK_NOPE = 64
QK_ROPE = 32
QK_HEAD = QK_NOPE + QK_ROPE
V_HEAD = 64
Q_LORA = 256
KV_LORA = 128
ROPE_THETA = 10000.0
Q_BLOCK = 128
CONV_CH = 512
CONV_WIDTH = 31
SG_CH = 512
SG_GROUPS = 4
SG_GROUP_CH = SG_CH // SG_GROUPS
CHUNK = 128
D_FF = 2816
FFN_CONV_WIDTH = 3
N_BRANCH = 3
EPS = 1e-6
IN_SIZES = (Q_LORA, KV_LORA, QK_ROPE, 2 * CONV_CH, 2 * SG_CH, N_BRANCH * D_MODEL)
D_IN = Q_LORA + KV_LORA + QK_ROPE + 2 * CONV_CH + 2 * SG_CH + N_BRANCH * D_MODEL

kernel_name = 'hybrid_mla_conformer_sgu_encoder'


def split_cols(z, sizes):
    outs, start = [], 0
    for n in sizes:
        outs.append(z[..., start:start + n])
        start += n
    return outs


def rms_norm(x, g):
    xf = x.astype(jnp.float32)
    y = xf * lax.rsqrt(jnp.mean(xf * xf, axis=-1, keepdims=True) + EPS)
    return (y * g.astype(jnp.float32)).astype(x.dtype)


def layer_norm(x, g, b):
    xf = x.astype(jnp.float32)
    mu = jnp.mean(xf, axis=-1, keepdims=True)
    var = jnp.mean(jnp.square(xf - mu), axis=-1, keepdims=True)
    y = (xf - mu) * lax.rsqrt(var + EPS)
    return (y * g.astype(jnp.float32) + b.astype(jnp.float32)).astype(x.dtype)


def rope_tables(seq_len):
    inv_freq = jnp.power(ROPE_THETA, -jnp.arange(0, QK_ROPE, 2, dtype=jnp.float32) / QK_ROPE)
    ang = jnp.arange(seq_len, dtype=jnp.float32)[:, None] * inv_freq[None, :]
    return jnp.cos(ang), jnp.sin(ang)


def apply_rope(x, cos, sin):
    x1, x2 = jnp.split(x.astype(jnp.float32), 2, axis=-1)
    c = cos[None, :, None, :]
    s = sin[None, :, None, :]
    return jnp.concatenate([x1 * c - x2 * s, x1 * s + x2 * c], axis=-1).astype(x.dtype)


def depthwise_conv(x, w, b):
    y = lax.conv_general_dilated(x, w[:, None, :].astype(x.dtype), window_strides=(1,), padding='SAME',
                                 dimension_numbers=('NWC', 'WIO', 'NWC'), feature_group_count=x.shape[-1])
    return y + b.astype(x.dtype)


def modulate(h, shift, scale):
    return h * (1 + scale[:, None, :]) + shift[:, None, :]


def block_attention(q, k, v):
    bsz, seq, _, _ = q.shape
    nb = seq // Q_BLOCK
    qb = q.reshape(bsz, nb, Q_BLOCK, N_HEADS, QK_HEAD).transpose(1, 0, 2, 3, 4)
    scale = QK_HEAD ** -0.5

    def attend(q_blk):
        s = jnp.einsum('bqhd,bkhd->bhqk', q_blk, k, preferred_element_type=jnp.float32) * scale
        p = jax.nn.softmax(s, axis=-1)
        return jnp.einsum('bhqk,bkhd->bqhd', p.astype(v.dtype), v)

    o = lax.map(attend, qb)
    return o.transpose(1, 0, 2, 3, 4).reshape(bsz, seq, N_HEADS * V_HEAD)


def mla_branch(c_q, c_kv, k_r, cos, sin, q_a_norm, w_uq, kv_a_norm, w_ukv, q_head_norm, k_head_norm, w_o):
    bsz, seq, _ = c_q.shape
    q = (rms_norm(c_q, q_a_norm) @ w_uq).reshape(bsz, seq, N_HEADS, QK_HEAD)
    kv = (rms_norm(c_kv, kv_a_norm) @ w_ukv).reshape(bsz, seq, N_HEADS, QK_NOPE + V_HEAD)
    k_nope, v = kv[..., :QK_NOPE], kv[..., QK_NOPE:]
    k_rope = jnp.broadcast_to(k_r[:, :, None, :], (bsz, seq, N_HEADS, QK_ROPE))
    k = jnp.concatenate([k_nope, k_rope], axis=-1)
    q = rms_norm(q, q_head_norm)
    k = rms_norm(k, k_head_norm)
    q = jnp.concatenate([q[..., :QK_NOPE], apply_rope(q[..., QK_NOPE:], cos, sin)], axis=-1)
    k = jnp.concatenate([k[..., :QK_NOPE], apply_rope(k[..., QK_NOPE:], cos, sin)], axis=-1)
    return block_attention(q, k, v) @ w_o


def conv_branch(conv_in, w_dw, b_dw, ln_g, ln_b, w_o):
    a, g = jnp.split(conv_in, 2, axis=-1)
    h = a * jax.nn.sigmoid(g)
    h = depthwise_conv(h, w_dw, b_dw)
    h = jax.nn.silu(layer_norm(h, ln_g, ln_b))
    return h @ w_o


def sgu_branch(sg_in, ln_g, ln_b, w_s, b_s, w_o):
    bsz, seq, _ = sg_in.shape
    z = jax.nn.gelu(sg_in)
    u, v = jnp.split(z, 2, axis=-1)
    v = layer_norm(v, ln_g, ln_b)
    vc = v.reshape(bsz, seq // CHUNK, CHUNK, SG_GROUPS, SG_GROUP_CH)
    s = jnp.einsum('gpq,bnqgc->bnpgc', w_s.astype(v.dtype), vc) + b_s.T[:, :, None].astype(v.dtype)
    return (u * s.reshape(bsz, seq, SG_CH)) @ w_o


def conv_ffn(h, w_up, w_dw, b_dw, w_down):
    z = depthwise_conv(h @ w_up, w_dw, b_dw)
    g, v = jnp.split(z, 2, axis=-1)
    return (jax.nn.silu(g) * v) @ w_down


def setup_inputs(seed: int = 0) -> dict:
    key = jax.random.key(seed)
    ks = jax.random.split(key, 31)
    L, D = DEPTH, D_MODEL

    def nrm(k, shape, scale):
        return jax.random.normal(k, shape, jnp.float32) * scale

    return {
        'x_prompt': nrm(ks[0], (BATCH, SEQ, D), 1.0),
        'x_sample': nrm(ks[1], (DEC_BATCH, DEC_SEQ, D), 1.0),
        'c_prompt': nrm(ks[2], (BATCH, D), 1.0),
        'c_sample': nrm(ks[3], (DEC_BATCH, D), 1.0),
        'w_ada': nrm(ks[4], (L, D, 6 * D), 0.5 * D ** -0.5),
        'b_ada': nrm(ks[5], (L, 6 * D), 0.01),
        'norm1': 1.0 + nrm(ks[6], (L, D), 0.01),
        'w_in': nrm(ks[7], (L, D, D_IN), D ** -0.5),
        'q_a_norm': 1.0 + nrm(ks[8], (L, Q_LORA), 0.01),
        'w_uq': nrm(ks[9], (L, Q_LORA, N_HEADS * QK_HEAD), Q_LORA ** -0.5),
        'kv_a_norm': 1.0 + nrm(ks[10], (L, KV_LORA), 0.01),
        'w_ukv': nrm(ks[11], (L, KV_LORA, N_HEADS * (QK_NOPE + V_HEAD)), KV_LORA ** -0.5),
        'q_head_norm': 1.0 + nrm(ks[12], (L, QK_HEAD), 0.01),
        'k_head_norm': 1.0 + nrm(ks[13], (L, QK_HEAD), 0.01),
        'w_attn_o': nrm(ks[14], (L, N_HEADS * V_HEAD, D), (N_HEADS * V_HEAD) ** -0.5),
        'conv_dw': nrm(ks[15], (L, CONV_WIDTH, CONV_CH), CONV_WIDTH ** -0.5),
        'conv_dw_b': nrm(ks[16], (L, CONV_CH), 0.01),
        'conv_ln_g': 1.0 + nrm(ks[17], (L, CONV_CH), 0.01),
        'conv_ln_b': nrm(ks[18], (L, CONV_CH), 0.01),
        'w_conv_o': nrm(ks[19], (L, CONV_CH, D), CONV_CH ** -0.5),
        'sg_ln_g': 1.0 + nrm(ks[20], (L, SG_CH), 0.01),
        'sg_ln_b': nrm(ks[21], (L, SG_CH), 0.01),
        'sg_w': nrm(ks[22], (L, SG_GROUPS, CHUNK, CHUNK), CHUNK ** -0.5),
        'sg_b': 1.0 + nrm(ks[23], (L, SG_GROUPS, CHUNK), 0.01),
        'w_sg_o': nrm(ks[24], (L, SG_CH, D), SG_CH ** -0.5),
        'w_out': nrm(ks[25], (L, D, D), D ** -0.5),
        'norm2': 1.0 + nrm(ks[26], (L, D), 0.01),
        'w_up': nrm(ks[27], (L, D, 2 * D_FF), D ** -0.5),
        'ffn_dw': nrm(ks[28], (L, FFN_CONV_WIDTH, 2 * D_FF), FFN_CONV_WIDTH ** -0.5),
        'ffn_dw_b': nrm(ks[29], (L, 2 * D_FF), 0.01),
        'w_down': nrm(ks[30], (L, D_FF, D), D_FF ** -0.5),
    }


def reference(x_prompt, x_sample, c_prompt, c_sample, w_ada, b_ada, norm1, w_in, q_a_norm, w_uq, kv_a_norm, w_ukv,
              q_head_norm, k_head_norm, w_attn_o, conv_dw, conv_dw_b, conv_ln_g, conv_ln_b, w_conv_o,
              sg_ln_g, sg_ln_b, sg_w, sg_b, w_sg_o, w_out, norm2, w_up, ffn_dw, ffn_dw_b, w_down):
    def encode(x, c):
        cos, sin = rope_tables(x.shape[1])
        for l in range(DEPTH):
            mod = jax.nn.silu(c) @ w_ada[l] + b_ada[l]
            sh1, sc1, g1, sh2, sc2, g2 = jnp.split(mod, 6, axis=-1)
            h = modulate(rms_norm(x, norm1[l]), sh1, sc1)
            c_q, c_kv, k_r, conv_in, sg_in, gate_in = split_cols(h @ w_in[l], IN_SIZES)
            y_attn = mla_branch(c_q, c_kv, k_r, cos, sin, q_a_norm[l], w_uq[l], kv_a_norm[l], w_ukv[l],
                                q_head_norm[l], k_head_norm[l], w_attn_o[l])
            y_conv = conv_branch(conv_in, conv_dw[l], conv_dw_b[l], conv_ln_g[l], conv_ln_b[l], w_conv_o[l])
            y_sg = sgu_branch(sg_in, sg_ln_g[l], sg_ln_b[l], sg_w[l], sg_b[l], w_sg_o[l])
            ga, gc, gs = jnp.split(jax.nn.sigmoid(gate_in), N_BRANCH, axis=-1)
            merged = ga * y_attn + gc * y_conv + gs * y_sg
            x = x + g1[:, None, :] * (merged @ w_out[l])
            h2 = modulate(rms_norm(x, norm2[l]), sh2, sc2)
            x = x + g2[:, None, :] * conv_ffn(h2, w_up[l], ffn_dw[l], ffn_dw_b[l], w_down[l])
        return x

    y_prompt = encode(x_prompt, c_prompt)
    y_sample = encode(x_sample, c_sample)
    return (y_prompt, y_sample)
```

```python
import functools
import math

import jax
import jax.numpy as jnp
from jax import lax
from jax.experimental import pallas as pl
from jax.experimental.pallas import tpu as pltpu

D_MODEL = 1024
DEPTH = 2
N_HEADS = 8
QK_NOPE = 64
QK_ROPE = 32
QK_HEAD = QK_NOPE + QK_ROPE
V_HEAD = 64
Q_LORA = 256
KV_LORA = 128
ROPE_THETA = 10000.0
CONV_CH = 512
CONV_WIDTH = 31
SG_CH = 512
SG_GROUPS = 4
CHUNK = 128
D_FF = 2816
FFN_CONV_WIDTH = 3
N_BRANCH = 3
EPS = 1e-6

LANES = 128
HALO = 16
TOKEN_TILE = 512
Q_TILE = 512
KV_CHUNK = 512
FF_CHUNK = 256
VMEM_LIMIT_BYTES = 56 * 1024 * 1024
MOD_ROWS = 40
MOD_COLS = 1536

BF16 = jnp.bfloat16
F32 = jnp.float32


def _const_spec(shape):
    zeros = (0,) * len(shape)
    return pl.BlockSpec(shape, lambda *_: zeros, pipeline_mode=pl.Buffered(1))


def _params(n_axes):
    return pltpu.CompilerParams(dimension_semantics=("parallel",) * n_axes,
                                vmem_limit_bytes=VMEM_LIMIT_BYTES)


def _rms(x):
    return x * lax.rsqrt(jnp.mean(x * x, axis=-1, keepdims=True) + EPS)


def _layer_norm(x, g, b):
    mu = jnp.mean(x, axis=-1, keepdims=True)
    xc = x - mu
    var = jnp.mean(xc * xc, axis=-1, keepdims=True)
    return xc * lax.rsqrt(var + EPS) * g + b


def _dot(a, b):
    return jnp.dot(a, b, preferred_element_type=F32)


def _mod_kernel(c_ref, w_ref, b_ref, o_ref):
    c = c_ref[...]
    a = (c * jax.nn.sigmoid(c)).astype(BF16)
    o_ref[0] = _dot(a, w_ref[0].astype(BF16)) + b_ref[0]


def _modulation(c_all, w_ada, b_ada):
    n_col = 6 * D_MODEL // MOD_COLS
    return pl.pallas_call(
        _mod_kernel,
        grid=(DEPTH, n_col),
        in_specs=[
            pl.BlockSpec((MOD_ROWS, D_MODEL), lambda l, j: (0, 0)),
            pl.BlockSpec((1, D_MODEL, MOD_COLS), lambda l, j: (l, 0, j)),
            pl.BlockSpec((1, 1, MOD_COLS), lambda l, j: (l, 0, j)),
        ],
        out_specs=pl.BlockSpec((1, MOD_ROWS, MOD_COLS), lambda l, j: (l, 0, j)),
        out_shape=jax.ShapeDtypeStruct((DEPTH, MOD_ROWS, 6 * D_MODEL), F32),
        compiler_params=_params(2),
        name="adaln_mod",
    )(c_all, w_ada, b_ada.reshape(DEPTH, 1, 6 * D_MODEL))


def _head_norm_rope(z, gain, cf, sa, sb, out_scale):
    r = lax.rsqrt(jnp.sum(z * z, axis=-1, keepdims=True) * (1.0 / QK_HEAD) + EPS)
    y = z * gain
    rot = y * cf + pltpu.roll(y, LANES - QK_ROPE // 2, 1) * sa + pltpu.roll(y, QK_ROPE // 2, 1) * sb
    return rot * (r * out_scale)


def _pre_kernel(xp_ref, x_ref, xn_ref, mod_ref, norm1_ref, wa_ref, wconv_ref, wsg_ref, wgate_ref,
                qan_ref, wuq_ref, kvan_ref, wuk_ref, wuv_ref, gq_ref, gk_ref, cf_ref, sa_ref, sb_ref,
                cw_ref, cb_ref, clg_ref, clb_ref, slg_ref, slb_ref, sw_ref, sbias_ref,
                q_ref, k_ref, v_ref, yc_ref, ys_ref, gate_ref, glu_scr, *, tiles_per_seq, q_scale):
    tm = x_ref.shape[0]
    pos = pl.program_id(0) % tiles_per_seq
    first = pos == 0
    last = pos == tiles_per_seq - 1

    xe = jnp.concatenate([xp_ref[...], x_ref[...], xn_ref[...]], axis=0)
    shift = mod_ref[0, 0:1, :]
    scale = mod_ref[0, 1:2, :]
    h = _rms(xe) * norm1_ref[...]
    h = h * (1.0 + scale) + shift
    hb = h.astype(BF16)
    hb_t = hb[HALO:HALO + tm]

    zc = _dot(hb, wconv_ref[...])
    glu = zc[:, :CONV_CH] * jax.nn.sigmoid(zc[:, CONV_CH:])
    row = lax.broadcasted_iota(jnp.int32, (tm + 2 * HALO, 1), 0)
    inside = jnp.logical_and(jnp.logical_or(row >= HALO, jnp.logical_not(first)),
                             jnp.logical_or(row < tm + HALO, jnp.logical_not(last)))
    glu_scr[...] = jnp.where(inside, glu, 0.0)
    acc = jnp.broadcast_to(cb_ref[...], (tm, CONV_CH))
    pad = CONV_WIDTH // 2
    for j in range(CONV_WIDTH):
        acc = acc + cw_ref[j:j + 1, :] * glu_scr[pl.ds(HALO - pad + j, tm), :]
    yc = _layer_norm(acc, clg_ref[...], clb_ref[...])
    yc_ref[...] = (yc * jax.nn.sigmoid(yc)).astype(BF16)

    za = _dot(hb_t, wa_ref[...])
    cq = (_rms(za[:, :Q_LORA]) * qan_ref[...]).astype(BF16)
    ckv = (_rms(za[:, Q_LORA:Q_LORA + KV_LORA]) * kvan_ref[...]).astype(BF16)
    kr = za[:, Q_LORA + KV_LORA:]
    cf = cf_ref[...]
    sa = sa_ref[...]
    sb = sb_ref[...]
    zq = _dot(cq, wuq_ref[...])
    zk = _dot(ckv, wuk_ref[...])
    for hd in range(N_HEADS):
        sl = slice(hd * LANES, (hd + 1) * LANES)
        q_ref[:, sl] = _head_norm_rope(zq[:, sl], gq_ref[...], cf, sa, sb, q_scale).astype(BF16)
        k_ref[:, sl] = _head_norm_rope(zk[:, sl] + kr, gk_ref[...], cf, sa, sb, 1.0).astype(BF16)
    v_ref[...] = _dot(ckv, wuv_ref[...]).astype(BF16)

    zs = jax.nn.gelu(_dot(hb_t, wsg_ref[...]))
    u = zs[:, :SG_CH]
    vb = _layer_norm(zs[:, SG_CH:], slg_ref[...], slb_ref[...]).astype(BF16)
    gch = SG_CH // SG_GROUPS
    for n in range(tm // CHUNK):
        rows = slice(n * CHUNK, (n + 1) * CHUNK)
        for g in range(SG_GROUPS):
            cols = slice(g * gch, (g + 1) * gch)
            s = _dot(sw_ref[g], vb[rows, cols]) + sbias_ref[:, cols]
            ys_ref[rows, cols] = (u[rows, cols] * s).astype(BF16)

    gate_ref[...] = _dot(hb_t, wgate_ref[...]).astype(BF16)


def _pre(x, mod3, lw, rope, seq):
    t = x.shape[0]
    tm = TOKEN_TILE
    tps = seq // tm
    nh = tm // HALO
    last_halo = t // HALO - 1
    kern = functools.partial(_pre_kernel, tiles_per_seq=tps,
                             q_scale=QK_HEAD ** -0.5 * math.log2(math.e))
    tile = lambda w: pl.BlockSpec((tm, w), lambda i: (i, 0))
    in_specs = [
        pl.BlockSpec((HALO, D_MODEL), lambda i: (jnp.maximum(i * nh - 1, 0), 0)),
        tile(D_MODEL),
        pl.BlockSpec((HALO, D_MODEL), lambda i: (jnp.minimum((i + 1) * nh, last_halo), 0)),
        pl.BlockSpec((1, 6, D_MODEL), lambda i: (i // tps, 0, 0)),
        _const_spec((1, D_MODEL)),
        _const_spec(lw["w_a"].shape), _const_spec(lw["w_conv"].shape), _const_spec(lw["w_sg"].shape),
        _const_spec(lw["w_gate"].shape),
        _const_spec((1, Q_LORA)), _const_spec(lw["w_uq"].shape),
        _const_spec((1, KV_LORA)), _const_spec(lw["w_uk"].shape), _const_spec(lw["w_uv"].shape),
        _const_spec((1, LANES)), _const_spec((1, LANES)),
        pl.BlockSpec((tm, LANES), lambda i: (i % tps, 0)),
        pl.BlockSpec((tm, LANES), lambda i: (i % tps, 0)),
        pl.BlockSpec((tm, LANES), lambda i: (i % tps, 0)),
        _const_spec((CONV_WIDTH, CONV_CH)), _const_spec((1, CONV_CH)), _const_spec((1, CONV_CH)),
        _const_spec((1, CONV_CH)),
        _const_spec((1, SG_CH)), _const_spec((1, SG_CH)), _const_spec((SG_GROUPS, CHUNK, CHUNK)),
        _const_spec((CHUNK, SG_CH)),
    ]
    widths = (N_HEADS * LANES, N_HEADS * LANES, N_HEADS * V_HEAD, CONV_CH, SG_CH, N_BRANCH * D_MODEL)
    return pl.pallas_call(
        kern,
        grid=(t // tm,),
        in_specs=in_specs,
        out_specs=[tile(w) for w in widths],
        out_shape=[jax.ShapeDtypeStruct((t, w), BF16) for w in widths],
        scratch_shapes=[pltpu.VMEM((tm + 2 * HALO, CONV_CH), F32)],
        compiler_params=_params(1),
        name="pre_mix",
    )(x, x, x, mod3, lw["norm1"], lw["w_a"], lw["w_conv"], lw["w_sg"], lw["w_gate"],
      lw["q_a_norm"], lw["w_uq"], lw["kv_a_norm"], lw["w_uk"], lw["w_uv"], lw["g_q"], lw["g_k"],
      rope[0], rope[1], rope[2],
      lw["conv_dw"], lw["conv_dw_b"], lw["conv_ln_g"], lw["conv_ln_b"],
      lw["sg_ln_g"], lw["sg_ln_b"], lw["sg_w"], lw["sg_bias"])


def _attn_kernel(q_ref, k_ref, v_ref, o_ref, *, n_chunks):
    tq = q_ref.shape[0]
    ck = KV_CHUNK
    low = lax.broadcasted_iota(jnp.int32, (tq, LANES), 1) < V_HEAD
    nt = (((1,), (1,)), ((), ()))
    for p in range(N_HEADS // 2):
        ce = slice(2 * p * LANES, (2 * p + 1) * LANES)
        co = slice((2 * p + 1) * LANES, (2 * p + 2) * LANES)
        cv = slice(p * LANES, (p + 1) * LANES)
        qe = q_ref[:, ce]
        qo = q_ref[:, co]

        def body(c, carry):
            me, le, mo, lo, acc = carry
            rows = pl.ds(pl.multiple_of(c * ck, ck), ck)
            vp = v_ref[rows, cv]
            se = lax.dot_general(qe, k_ref[rows, ce], nt, preferred_element_type=F32)
            so = lax.dot_general(qo, k_ref[rows, co], nt, preferred_element_type=F32)
            me_new = jnp.maximum(me, jnp.max(se, axis=-1, keepdims=True))
            mo_new = jnp.maximum(mo, jnp.max(so, axis=-1, keepdims=True))
            ae = jnp.exp2(me - me_new)
            ao = jnp.exp2(mo - mo_new)
            pe = jnp.exp2(se - me_new)
            po = jnp.exp2(so - mo_new)
            le = ae * le + jnp.sum(pe, axis=-1, keepdims=True)
            lo = ao * lo + jnp.sum(po, axis=-1, keepdims=True)
            pve = _dot(pe.astype(BF16), vp)
            pvo = _dot(po.astype(BF16), vp)
            acc = jnp.where(low, ae * acc + pve, ao * acc + pvo)
            return me_new, le, mo_new, lo, acc

        neg = jnp.full((tq, 1), -jnp.inf, F32)
        zero = jnp.zeros((tq, 1), F32)
        _, le, _, lo, acc = lax.fori_loop(0, n_chunks, body,
                                          (neg, zero, neg, zero, jnp.zeros((tq, LANES), F32)))
        o_ref[:, cv] = (acc * jnp.where(low, 1.0 / le, 1.0 / lo)).astype(BF16)


def _attention(q, k, v, batch, seq):
    tq = Q_TILE
    nq = seq // tq
    kv_mode = pl.Buffered(1) if seq * N_HEADS * (LANES + V_HEAD) * 2 * 2 > VMEM_LIMIT_BYTES // 2 else None
    kv_kwargs = {} if kv_mode is None else {"pipeline_mode": kv_mode}
    return pl.pallas_call(
        functools.partial(_attn_kernel, n_chunks=seq // KV_CHUNK),
        grid=(batch, nq),
        in_specs=[
            pl.BlockSpec((tq, N_HEADS * LANES), lambda b, i: (b * nq + i, 0)),
            pl.BlockSpec((seq, N_HEADS * LANES), lambda b, i: (b, 0), **kv_kwargs),
            pl.BlockSpec((seq, N_HEADS * V_HEAD), lambda b, i: (b, 0), **kv_kwargs),
        ],
        out_specs=pl.BlockSpec((tq, N_HEADS * V_HEAD), lambda b, i: (b * nq + i, 0)),
        out_shape=jax.ShapeDtypeStruct((batch * seq, N_HEADS * V_HEAD), BF16),
        compiler_params=_params(2),
        name="attention",
    )(q, k, v)


def _merge_kernel(o_ref, yc_ref, ys_ref, gate_ref, x_ref, mod_ref, wao_ref, wco_ref, wso_ref, wout_ref,
                  out_ref):
    gate = jax.nn.sigmoid(gate_ref[...].astype(F32))
    merged = (gate[:, :D_MODEL] * _dot(o_ref[...], wao_ref[...])
              + gate[:, D_MODEL:2 * D_MODEL] * _dot(yc_ref[...], wco_ref[...])
              + gate[:, 2 * D_MODEL:] * _dot(ys_ref[...], wso_ref[...]))
    y = _dot(merged.astype(BF16), wout_ref[...])
    out_ref[...] = x_ref[...] + mod_ref[0, 2:3, :] * y


def _merge(o, yc, ys, gate, x, mod3, lw, seq):
    t = x.shape[0]
    tm = TOKEN_TILE
    tps = seq // tm
    tile = lambda w: pl.BlockSpec((tm, w), lambda i: (i, 0))
    return pl.pallas_call(
        _merge_kernel,
        grid=(t // tm,),
        in_specs=[tile(N_HEADS * V_HEAD), tile(CONV_CH), tile(SG_CH), tile(N_BRANCH * D_MODEL), tile(D_MODEL),
                  pl.BlockSpec((1, 6, D_MODEL), lambda i: (i // tps, 0, 0)),
                  _const_spec(lw["w_attn_o"].shape), _const_spec(lw["w_conv_o"].shape),
                  _const_spec(lw["w_sg_o"].shape), _const_spec(lw["w_out"].shape)],
        out_specs=tile(D_MODEL),
        out_shape=jax.ShapeDtypeStruct((t, D_MODEL), F32),
        compiler_params=_params(1),
        name="merge_out",
    )(o, yc, ys, gate, x, mod3, lw["w_attn_o"], lw["w_conv_o"], lw["w_sg_o"], lw["w_out"])


def _ffn_kernel(xp_ref, x_ref, xn_ref, mod_ref, norm2_ref, wup_ref, dw_ref, dwb_ref, wdown_ref,
                out_ref, a_scr, *, tiles_per_seq):
    tm = x_ref.shape[0]
    te = tm + 2 * HALO
    pos = pl.program_id(0) % tiles_per_seq
    first = pos == 0
    last = pos == tiles_per_seq - 1

    x = x_ref[...]
    xe = jnp.concatenate([xp_ref[...], x, xn_ref[...]], axis=0)
    h = _rms(xe) * norm2_ref[...]
    h = h * (1.0 + mod_ref[0, 4:5, :]) + mod_ref[0, 3:4, :]
    row = lax.broadcasted_iota(jnp.int32, (te, 1), 0)
    inside = jnp.logical_and(jnp.logical_or(row >= HALO, jnp.logical_not(first)),
                             jnp.logical_or(row < tm + HALO, jnp.logical_not(last)))
    hb = jnp.where(inside, h, 0.0).astype(BF16)

    def conv3(z, cols):
        w = dw_ref[:, cols]
        prev = pltpu.roll(z, 1, 0)[HALO:HALO + tm]
        nxt = pltpu.roll(z, te - 1, 0)[HALO:HALO + tm]
        return w[0:1] * prev + w[1:2] * z[HALO:HALO + tm] + w[2:3] * nxt + dwb_ref[:, cols]

    for c in range(D_FF // FF_CHUNK):
        cg = slice(c * FF_CHUNK, (c + 1) * FF_CHUNK)
        cv = slice(D_FF + c * FF_CHUNK, D_FF + (c + 1) * FF_CHUNK)
        g = conv3(_dot(hb, wup_ref[:, cg]), cg)
        v = conv3(_dot(hb, wup_ref[:, cv]), cv)
        a_scr[:, cg] = (g * jax.nn.sigmoid(g) * v).astype(BF16)
    y = _dot(a_scr[...], wdown_ref[...])
    out_ref[...] = x + mod_ref[0, 5:6, :] * y


def _ffn(x, mod3, lw, seq):
    t = x.shape[0]
    tm = TOKEN_TILE
    tps = seq // tm
    nh = tm // HALO
    last_halo = t // HALO - 1
    return pl.pallas_call(
        functools.partial(_ffn_kernel, tiles_per_seq=tps),
        grid=(t // tm,),
        in_specs=[
            pl.BlockSpec((HALO, D_MODEL), lambda i: (jnp.maximum(i * nh - 1, 0), 0)),
            pl.BlockSpec((tm, D_MODEL), lambda i: (i, 0)),
            pl.BlockSpec((HALO, D_MODEL), lambda i: (jnp.minimum((i + 1) * nh, last_halo), 0)),
            pl.BlockSpec((1, 6, D_MODEL), lambda i: (i // tps, 0, 0)),
            _const_spec((1, D_MODEL)),
            _const_spec((D_MODEL, 2 * D_FF)), _const_spec((FFN_CONV_WIDTH, 2 * D_FF)),
            _const_spec((1, 2 * D_FF)), _const_spec((D_FF, D_MODEL)),
        ],
        out_specs=pl.BlockSpec((tm, D_MODEL), lambda i: (i, 0)),
        out_shape=jax.ShapeDtypeStruct((t, D_MODEL), F32),
        scratch_shapes=[pltpu.VMEM((tm, D_FF), BF16)],
        compiler_params=_params(1),
        name="conv_ffn",
    )(x, x, x, mod3, lw["norm2"], lw["w_up"], lw["ffn_dw"], lw["ffn_dw_b"], lw["w_down"])


def _pad_heads(w, real):
    lead = w.shape[:-1]
    w = w.reshape(lead + (N_HEADS, real))
    w = jnp.pad(w, [(0, 0)] * len(lead) + [(0, 0), (0, LANES - real)])
    return w.reshape(lead + (N_HEADS * LANES,))


def _rope_tables(seq):
    half = QK_ROPE // 2
    inv_freq = jnp.power(ROPE_THETA, -jnp.arange(0, QK_ROPE, 2, dtype=F32) / QK_ROPE)
    ang = jnp.arange(seq, dtype=F32)[:, None] * inv_freq[None, :]
    cos, sin = jnp.cos(ang), jnp.sin(ang)
    z = lambda n: jnp.zeros((seq, n), F32)
    cf = jnp.concatenate([jnp.ones((seq, QK_NOPE), F32), cos, cos, z(LANES - QK_HEAD)], axis=1)
    sa = jnp.concatenate([z(QK_NOPE), -sin, z(half), z(LANES - QK_HEAD)], axis=1)
    sb = jnp.concatenate([z(QK_NOPE), z(half), sin, z(LANES - QK_HEAD)], axis=1)
    return cf, sa, sb


def _layer_weights(l, w_in, p):
    o_q, o_kv, o_kr, o_conv, o_sg, o_gate = 0, Q_LORA, Q_LORA + KV_LORA, Q_LORA + KV_LORA + QK_ROPE, \
        Q_LORA + KV_LORA + QK_ROPE + 2 * CONV_CH, Q_LORA + KV_LORA + QK_ROPE + 2 * CONV_CH + 2 * SG_CH
    wi = w_in[l]
    kr_block = jnp.pad(wi[:, o_kr:o_conv], ((0, 0), (QK_NOPE, LANES - QK_HEAD)))
    w_ukv = p["w_ukv"][l].reshape(KV_LORA, N_HEADS, QK_NOPE + V_HEAD)
    row = lambda a: a[l].reshape(1, -1)
    return {
        "norm1": row(p["norm1"]), "norm2": row(p["norm2"]),
        "w_a": jnp.concatenate([wi[:, o_q:o_kr], kr_block], axis=1).astype(BF16),
        "w_conv": wi[:, o_conv:o_sg].astype(BF16),
        "w_sg": wi[:, o_sg:o_gate].astype(BF16),
        "w_gate": wi[:, o_gate:].astype(BF16),
        "q_a_norm": row(p["q_a_norm"]), "kv_a_norm": row(p["kv_a_norm"]),
        "w_uq": _pad_heads(p["w_uq"][l], QK_HEAD).astype(BF16),
        "w_uk": _pad_heads(w_ukv[:, :, :QK_NOPE].reshape(KV_LORA, N_HEADS * QK_NOPE), QK_NOPE).astype(BF16),
        "w_uv": w_ukv[:, :, QK_NOPE:].reshape(KV_LORA, N_HEADS * V_HEAD).astype(BF16),
        "g_q": jnp.pad(p["q_head_norm"][l], (0, LANES - QK_HEAD)).reshape(1, LANES),
        "g_k": jnp.pad(p["k_head_norm"][l], (0, LANES - QK_HEAD)).reshape(1, LANES),
        "w_attn_o": p["w_attn_o"][l].astype(BF16),
        "conv_dw": p["conv_dw"][l], "conv_dw_b": row(p["conv_dw_b"]),
        "conv_ln_g": row(p["conv_ln_g"]), "conv_ln_b": row(p["conv_ln_b"]),
        "w_conv_o": p["w_conv_o"][l].astype(BF16),
        "sg_ln_g": row(p["sg_ln_g"]), "sg_ln_b": row(p["sg_ln_b"]),
        "sg_w": p["sg_w"][l].astype(BF16),
        "sg_bias": jnp.repeat(p["sg_b"][l].T, SG_CH // SG_GROUPS, axis=1),
        "w_sg_o": p["w_sg_o"][l].astype(BF16),
        "w_out": p["w_out"][l].astype(BF16),
        "w_up": p["w_up"][l].astype(BF16),
        "ffn_dw": p["ffn_dw"][l], "ffn_dw_b": row(p["ffn_dw_b"]),
        "w_down": p["w_down"][l].astype(BF16),
    }


def _encode(x, mod, layers, batch, seq):
    rope = _rope_tables(seq)
    x = x.reshape(batch * seq, D_MODEL)
    for l in range(DEPTH):
        lw = layers[l]
        mod3 = mod[l].reshape(batch, 6, D_MODEL)
        q, k, v, yc, ys, gate = _pre(x, mod3, lw, rope, seq)
        o = _attention(q, k, v, batch, seq)
        x = _merge(o, yc, ys, gate, x, mod3, lw, seq)
        x = _ffn(x, mod3, lw, seq)
    return x.reshape(batch, seq, D_MODEL)


def kernel(x_prompt, x_sample, c_prompt, c_sample, w_ada, b_ada, norm1, w_in, q_a_norm, w_uq, kv_a_norm, w_ukv,
           q_head_norm, k_head_norm, w_attn_o, conv_dw, conv_dw_b, conv_ln_g, conv_ln_b, w_conv_o,
           sg_ln_g, sg_ln_b, sg_w, sg_b, w_sg_o, w_out, norm2, w_up, ffn_dw, ffn_dw_b, w_down):
    p = dict(norm1=norm1, q_a_norm=q_a_norm, w_uq=w_uq, kv_a_norm=kv_a_norm, w_ukv=w_ukv,
             q_head_norm=q_head_norm, k_head_norm=k_head_norm, w_attn_o=w_attn_o, conv_dw=conv_dw,
             conv_dw_b=conv_dw_b, conv_ln_g=conv_ln_g, conv_ln_b=conv_ln_b, w_conv_o=w_conv_o,
             sg_ln_g=sg_ln_g, sg_ln_b=sg_ln_b, sg_w=sg_w, sg_b=sg_b, w_sg_o=w_sg_o, w_out=w_out,
             norm2=norm2, w_up=w_up, ffn_dw=ffn_dw, ffn_dw_b=ffn_dw_b, w_down=w_down)
    layers = [_layer_weights(l, w_in, p) for l in range(DEPTH)]

    bp, sp, _ = x_prompt.shape
    bs, ss, _ = x_sample.shape
    c_all = jnp.concatenate([c_prompt, c_sample, jnp.zeros((MOD_ROWS - bp - bs, D_MODEL), F32)], axis=0)
    mod = _modulation(c_all, w_ada, b_ada)
    y_prompt = _encode(x_prompt, mod[:, :bp], layers, bp, sp)
    y_sample = _encode(x_sample, mod[:, bp:bp + bs], layers, bs, ss)
    return (y_prompt, y_sample)
```

```python
import functools
import math

import jax
import jax.numpy as jnp
from jax import lax
from jax.experimental import pallas as pl
from jax.experimental.pallas import tpu as pltpu

D_MODEL = 1024
DEPTH = 2
N_HEADS = 8
QK_NOPE = 64
QK_ROPE = 32
QK_HEAD = QK_NOPE + QK_ROPE
V_HEAD = 64
Q_LORA = 256
KV_LORA = 128
ROPE_THETA = 10000.0
CONV_CH = 512
CONV_WIDTH = 31
SG_CH = 512
SG_GROUPS = 4
CHUNK = 128
D_FF = 2816
FFN_CONV_WIDTH = 3
N_BRANCH = 3
EPS = 1e-6

LANES = 128
SUBLANES = 8
HALO = 16
TOKEN_TILE = 512
Q_TILE = 256
KV_SEGMENT = 2048
KV_BLOCK = 512
PAIRS_PER_STEP = 2
FF_CHUNK = 256
VMEM_LIMIT_BYTES = 56 * 1024 * 1024
MOD_ROWS = 40
MOD_COLS = 1536

BF16 = jnp.bfloat16
F32 = jnp.float32


def _const_spec(shape):
    zeros = (0,) * len(shape)
    return pl.BlockSpec(shape, lambda *_: zeros, pipeline_mode=pl.Buffered(1))


def _params(n_axes):
    return pltpu.CompilerParams(dimension_semantics=("parallel",) * n_axes,
                                vmem_limit_bytes=VMEM_LIMIT_BYTES)


def _rms(x):
    return x * lax.rsqrt(jnp.mean(x * x, axis=-1, keepdims=True) + EPS)


def _layer_norm(x, g, b):
    mu = jnp.mean(x, axis=-1, keepdims=True)
    xc = x - mu
    var = jnp.mean(xc * xc, axis=-1, keepdims=True)
    return xc * lax.rsqrt(var + EPS) * g + b


def _dot(a, b):
    return jnp.dot(a, b, preferred_element_type=F32)


def _mod_kernel(c_ref, w_ref, b_ref, o_ref):
    c = c_ref[...]
    a = (c * jax.nn.sigmoid(c)).astype(BF16)
    o_ref[0] = _dot(a, w_ref[0].astype(BF16)) + b_ref[0]


def _modulation(c_all, w_ada, b_ada):
    n_col = 6 * D_MODEL // MOD_COLS
    return pl.pallas_call(
        _mod_kernel,
        grid=(DEPTH, n_col),
        in_specs=[
            pl.BlockSpec((MOD_ROWS, D_MODEL), lambda l, j: (0, 0)),
            pl.BlockSpec((1, D_MODEL, MOD_COLS), lambda l, j: (l, 0, j)),
            pl.BlockSpec((1, 1, MOD_COLS), lambda l, j: (l, 0, j)),
        ],
        out_specs=pl.BlockSpec((1, MOD_ROWS, MOD_COLS), lambda l, j: (l, 0, j)),
        out_shape=jax.ShapeDtypeStruct((DEPTH, MOD_ROWS, 6 * D_MODEL), F32),
        compiler_params=_params(2),
        name="adaln_mod",
    )(c_all, w_ada, b_ada.reshape(DEPTH, 1, 6 * D_MODEL))


def _head_norm_rope(z, gain, cf, sa, sb, out_scale):
    r = lax.rsqrt(jnp.sum(z * z, axis=-1, keepdims=True) * (1.0 / QK_HEAD) + EPS)
    y = z * gain
    rot = y * cf + pltpu.roll(y, LANES - QK_ROPE // 2, 1) * sa + pltpu.roll(y, QK_ROPE // 2, 1) * sb
    return rot * (r * out_scale)


def _pre_kernel(xp_ref, x_ref, xn_ref, mod_ref, norm1_ref, wa_ref, wconv_ref, wsg_ref, wgate_ref,
                qan_ref, wuq_ref, kvan_ref, wuk_ref, wuv_ref, gq_ref, gk_ref, cf_ref, sa_ref, sb_ref,
                cw_ref, cb_ref, clg_ref, clb_ref, slg_ref, slb_ref, sw_ref, sbias_ref,
                q_ref, k_ref, v_ref, yc_ref, ys_ref, gate_ref, glu_scr, shift_scr, *, tiles_per_seq, q_scale):
    tm = x_ref.shape[0]
    pos = pl.program_id(0) % tiles_per_seq
    first = pos == 0
    last = pos == tiles_per_seq - 1

    xe = jnp.concatenate([xp_ref[...], x_ref[...], xn_ref[...]], axis=0)
    shift = mod_ref[0, 0:1, :]
    scale = mod_ref[0, 1:2, :]
    h = _rms(xe) * norm1_ref[...]
    h = h * (1.0 + scale) + shift
    hb = h.astype(BF16)
    hb_t = hb[HALO:HALO + tm]

    zc = _dot(hb, wconv_ref[...])
    glu = zc[:, :CONV_CH] * jax.nn.sigmoid(zc[:, CONV_CH:])
    row = lax.broadcasted_iota(jnp.int32, (tm + 2 * HALO, 1), 0)
    inside = jnp.logical_and(jnp.logical_or(row >= HALO, jnp.logical_not(first)),
                             jnp.logical_or(row < tm + HALO, jnp.logical_not(last)))
    glu_scr[...] = jnp.where(inside, glu, 0.0)
    span = tm + 2 * HALO - SUBLANES
    for ph in range(1, SUBLANES):
        shift_scr[ph - 1] = glu_scr[pl.ds(ph, span), :]
    acc = jnp.broadcast_to(cb_ref[...], (tm, CONV_CH))
    pad = CONV_WIDTH // 2
    for j in range(CONV_WIDTH):
        blk, ph = divmod(HALO - pad + j, SUBLANES)
        rows = slice(blk * SUBLANES, blk * SUBLANES + tm)
        tap = glu_scr[rows, :] if ph == 0 else shift_scr[ph - 1, rows, :]
        acc = acc + cw_ref[j:j + 1, :] * tap
    yc = _layer_norm(acc, clg_ref[...], clb_ref[...])
    yc_ref[...] = (yc * jax.nn.sigmoid(yc)).astype(BF16)

    za = _dot(hb_t, wa_ref[...])
    cq = (_rms(za[:, :Q_LORA]) * qan_ref[...]).astype(BF16)
    ckv = (_rms(za[:, Q_LORA:Q_LORA + KV_LORA]) * kvan_ref[...]).astype(BF16)
    kr = za[:, Q_LORA + KV_LORA:]
    cf = cf_ref[...]
    sa = sa_ref[...]
    sb = sb_ref[...]
    zq = _dot(cq, wuq_ref[...])
    zk = _dot(ckv, wuk_ref[...])
    for hd in range(N_HEADS):
        sl = slice(hd * LANES, (hd + 1) * LANES)
        q_ref[hd] = _head_norm_rope(zq[:, sl], gq_ref[...], cf, sa, sb, q_scale).astype(BF16)
        k_ref[hd] = _head_norm_rope(zk[:, sl] + kr, gk_ref[...], cf, sa, sb, 1.0).astype(BF16)
    vv = _dot(ckv, wuv_ref[...]).astype(BF16)
    for pr in range(N_HEADS // 2):
        v_ref[pr] = vv[:, pr * LANES:(pr + 1) * LANES]

    zs = jax.nn.gelu(_dot(hb_t, wsg_ref[...]))
    u = zs[:, :SG_CH]
    vb = _layer_norm(zs[:, SG_CH:], slg_ref[...], slb_ref[...]).astype(BF16)
    gch = SG_CH // SG_GROUPS
    for n in range(tm // CHUNK):
        rows = slice(n * CHUNK, (n + 1) * CHUNK)
        for g in range(SG_GROUPS):
            cols = slice(g * gch, (g + 1) * gch)
            s = _dot(sw_ref[g], vb[rows, cols]) + sbias_ref[:, cols]
            ys_ref[rows, cols] = (u[rows, cols] * s).astype(BF16)

    gate_ref[...] = _dot(hb_t, wgate_ref[...]).astype(BF16)


def _pre(x, mod3, lw, rope, seq):
    t = x.shape[0]
    tm = TOKEN_TILE
    tps = seq // tm
    nh = tm // HALO
    last_halo = t // HALO - 1
    kern = functools.partial(_pre_kernel, tiles_per_seq=tps,
                             q_scale=QK_HEAD ** -0.5 * math.log2(math.e))
    tile = lambda w: pl.BlockSpec((tm, w), lambda i: (i, 0))
    in_specs = [
        pl.BlockSpec((HALO, D_MODEL), lambda i: (jnp.maximum(i * nh - 1, 0), 0)),
        tile(D_MODEL),
        pl.BlockSpec((HALO, D_MODEL), lambda i: (jnp.minimum((i + 1) * nh, last_halo), 0)),
        pl.BlockSpec((1, 6, D_MODEL), lambda i: (i // tps, 0, 0)),
        _const_spec((1, D_MODEL)),
        _const_spec(lw["w_a"].shape), _const_spec(lw["w_conv"].shape), _const_spec(lw["w_sg"].shape),
        _const_spec(lw["w_gate"].shape),
        _const_spec((1, Q_LORA)), _const_spec(lw["w_uq"].shape),
        _const_spec((1, KV_LORA)), _const_spec(lw["w_uk"].shape), _const_spec(lw["w_uv"].shape),
        _const_spec((1, LANES)), _const_spec((1, LANES)),
        pl.BlockSpec((tm, LANES), lambda i: (i % tps, 0)),
        pl.BlockSpec((tm, LANES), lambda i: (i % tps, 0)),
        pl.BlockSpec((tm, LANES), lambda i: (i % tps, 0)),
        _const_spec((CONV_WIDTH, CONV_CH)), _const_spec((1, CONV_CH)), _const_spec((1, CONV_CH)),
        _const_spec((1, CONV_CH)),
        _const_spec((1, SG_CH)), _const_spec((1, SG_CH)), _const_spec((SG_GROUPS, CHUNK, CHUNK)),
        _const_spec((CHUNK, SG_CH)),
    ]
    heads = lambda n: pl.BlockSpec((n, tm, LANES), lambda i: (0, i, 0))
    widths = (CONV_CH, SG_CH, N_BRANCH * D_MODEL)
    n_pairs = N_HEADS // 2
    return pl.pallas_call(
        kern,
        grid=(t // tm,),
        in_specs=in_specs,
        out_specs=[heads(N_HEADS), heads(N_HEADS), heads(n_pairs)] + [tile(w) for w in widths],
        out_shape=[jax.ShapeDtypeStruct((N_HEADS, t, LANES), BF16),
                   jax.ShapeDtypeStruct((N_HEADS, t, LANES), BF16),
                   jax.ShapeDtypeStruct((n_pairs, t, LANES), BF16)]
                  + [jax.ShapeDtypeStruct((t, w), BF16) for w in widths],
        scratch_shapes=[pltpu.VMEM((tm + 2 * HALO, CONV_CH), F32),
                        pltpu.VMEM((SUBLANES - 1, tm + 2 * HALO - SUBLANES, CONV_CH), F32)],
        compiler_params=_params(1),
        name="pre_mix",
    )(x, x, x, mod3, lw["norm1"], lw["w_a"], lw["w_conv"], lw["w_sg"], lw["w_gate"],
      lw["q_a_norm"], lw["w_uq"], lw["kv_a_norm"], lw["w_uk"], lw["w_uv"], lw["g_q"], lw["g_k"],
      rope[0], rope[1], rope[2],
      lw["conv_dw"], lw["conv_dw_b"], lw["conv_ln_g"], lw["conv_ln_b"],
      lw["sg_ln_g"], lw["sg_ln_b"], lw["sg_w"], lw["sg_bias"])


def _attn_kernel(q_ref, k_ref, v_ref, o_ref, s_scr, p_scr, *, n_seg):
    tq = q_ref.shape[1]
    seg, kb = KV_SEGMENT, KV_BLOCK
    low = lax.broadcasted_iota(jnp.int32, (tq, LANES), 1) < V_HEAD
    nt = (((1,), (1,)), ((), ()))

    def scores(qh, head, base, slot):
        mx = None
        for j in range(seg // kb):
            kj = k_ref[head, pl.ds(pl.multiple_of(base + j * kb, kb), kb), :]
            sj = lax.dot_general(qh, kj, nt, preferred_element_type=F32)
            s_scr[slot, :, j * kb:(j + 1) * kb] = sj
            for t in range(kb // LANES):
                blk = sj[:, t * LANES:(t + 1) * LANES]
                mx = blk if mx is None else jnp.maximum(mx, blk)
        return jnp.max(mx, axis=-1, keepdims=True)

    def probs(slot, m):
        tot = None
        for j in range(seg // kb):
            pj = jnp.exp2(s_scr[slot, :, j * kb:(j + 1) * kb] - m)
            p_scr[slot, :, j * kb:(j + 1) * kb] = pj.astype(BF16)
            for t in range(kb // LANES):
                blk = pj[:, t * LANES:(t + 1) * LANES]
                tot = blk if tot is None else tot + blk
        return jnp.sum(tot, axis=-1, keepdims=True)

    nh = 2 * PAIRS_PER_STEP

    def group(gp, _):
        heads = [nh * gp + i for i in range(nh)]
        qs = [q_ref[h] for h in heads]

        def segment(g, carry):
            base = g * seg
            m_loc = [scores(qs[i], heads[i], base, i) for i in range(nh)]
            vrows = pl.ds(pl.multiple_of(base, seg), seg)
            vps = [v_ref[PAIRS_PER_STEP * gp + i // 2, vrows, :] for i in range(nh)]
            if n_seg == 1:
                l_new = [probs(i, m_loc[i]) for i in range(nh)]
                pv = [_dot(p_scr[i], vps[i]) for i in range(nh)]
                accs = [jnp.where(low, pv[2 * r], pv[2 * r + 1]) for r in range(PAIRS_PER_STEP)]
                return tuple(m_loc), tuple(l_new), tuple(accs)
            ms, ls, accs = carry
            m_new = [jnp.maximum(ms[i], m_loc[i]) for i in range(nh)]
            alpha = [jnp.exp2(ms[i] - m_new[i]) for i in range(nh)]
            l_new = [alpha[i] * ls[i] + probs(i, m_new[i]) for i in range(nh)]
            pv = [_dot(p_scr[i], vps[i]) for i in range(nh)]
            accs = [jnp.where(low, alpha[2 * r] * accs[r] + pv[2 * r],
                              alpha[2 * r + 1] * accs[r] + pv[2 * r + 1]) for r in range(PAIRS_PER_STEP)]
            return tuple(m_new), tuple(l_new), tuple(accs)

        neg = jnp.full((tq, 1), -jnp.inf, F32)
        zero = jnp.zeros((tq, 1), F32)
        init = ((neg,) * nh, (zero,) * nh, (jnp.zeros((tq, LANES), F32),) * PAIRS_PER_STEP)
        _, ls, accs = segment(0, init) if n_seg == 1 else lax.fori_loop(0, n_seg, segment, init)
        for r in range(PAIRS_PER_STEP):
            inv = jnp.where(low, 1.0 / ls[2 * r], 1.0 / ls[2 * r + 1])
            o_ref[PAIRS_PER_STEP * gp + r] = (accs[r] * inv).astype(BF16)
        return 0

    lax.fori_loop(0, N_HEADS // nh, group, 0)


def _attention(q, k, v, batch, seq):
    tq = Q_TILE
    nq = seq // tq
    n_pairs = N_HEADS // 2
    kv_bytes = seq * (N_HEADS + n_pairs) * LANES * 2
    kv_kwargs = {"pipeline_mode": pl.Buffered(1)} if 2 * kv_bytes > VMEM_LIMIT_BYTES // 2 else {}
    return pl.pallas_call(
        functools.partial(_attn_kernel, n_seg=seq // KV_SEGMENT),
        grid=(batch, nq),
        in_specs=[
            pl.BlockSpec((N_HEADS, tq, LANES), lambda b, i: (0, b * nq + i, 0)),
            pl.BlockSpec((N_HEADS, seq, LANES), lambda b, i: (0, b, 0), **kv_kwargs),
            pl.BlockSpec((n_pairs, seq, LANES), lambda b, i: (0, b, 0), **kv_kwargs),
        ],
        out_specs=pl.BlockSpec((n_pairs, tq, LANES), lambda b, i: (0, b * nq + i, 0)),
        out_shape=jax.ShapeDtypeStruct((n_pairs, batch * seq, LANES), BF16),
        scratch_shapes=[pltpu.VMEM((2 * PAIRS_PER_STEP, tq, KV_SEGMENT), F32),
                        pltpu.VMEM((2 * PAIRS_PER_STEP, tq, KV_SEGMENT), BF16)],
        compiler_params=_params(2),
        name="attention",
    )(q, k, v)


def _merge_kernel(o_ref, yc_ref, ys_ref, gate_ref, x_ref, mod_ref, wao_ref, wco_ref, wso_ref, wout_ref,
                  out_ref):
    gate = jax.nn.sigmoid(gate_ref[...].astype(F32))
    o = jnp.concatenate([o_ref[pr] for pr in range(N_HEADS // 2)], axis=1)
    merged = (gate[:, :D_MODEL] * _dot(o, wao_ref[...])
              + gate[:, D_MODEL:2 * D_MODEL] * _dot(yc_ref[...], wco_ref[...])
              + gate[:, 2 * D_MODEL:] * _dot(ys_ref[...], wso_ref[...]))
    y = _dot(merged.astype(BF16), wout_ref[...])
    out_ref[...] = x_ref[...] + mod_ref[0, 2:3, :] * y


def _merge(o, yc, ys, gate, x, mod3, lw, seq):
    t = x.shape[0]
    tm = TOKEN_TILE
    tps = seq // tm
    tile = lambda w: pl.BlockSpec((tm, w), lambda i: (i, 0))
    return pl.pallas_call(
        _merge_kernel,
        grid=(t // tm,),
        in_specs=[pl.BlockSpec((N_HEADS // 2, tm, LANES), lambda i: (0, i, 0)),
                  tile(CONV_CH), tile(SG_CH), tile(N_BRANCH * D_MODEL), tile(D_MODEL),
                  pl.BlockSpec((1, 6, D_MODEL), lambda i: (i // tps, 0, 0)),
                  _const_spec(lw["w_attn_o"].shape), _const_spec(lw["w_conv_o"].shape),
                  _const_spec(lw["w_sg_o"].shape), _const_spec(lw["w_out"].shape)],
        out_specs=tile(D_MODEL),
        out_shape=jax.ShapeDtypeStruct((t, D_MODEL), F32),
        compiler_params=_params(1),
        name="merge_out",
    )(o, yc, ys, gate, x, mod3, lw["w_attn_o"], lw["w_conv_o"], lw["w_sg_o"], lw["w_out"])


def _ffn_kernel(xp_ref, x_ref, xn_ref, mod_ref, norm2_ref, wup_ref, dw_ref, dwb_ref, wdown_ref,
                out_ref, a_scr, *, tiles_per_seq):
    tm = x_ref.shape[0]
    te = tm + 2 * HALO
    pos = pl.program_id(0) % tiles_per_seq
    first = pos == 0
    last = pos == tiles_per_seq - 1

    x = x_ref[...]
    xe = jnp.concatenate([xp_ref[...], x, xn_ref[...]], axis=0)
    h = _rms(xe) * norm2_ref[...]
    h = h * (1.0 + mod_ref[0, 4:5, :]) + mod_ref[0, 3:4, :]
    row = lax.broadcasted_iota(jnp.int32, (te, 1), 0)
    inside = jnp.logical_and(jnp.logical_or(row >= HALO, jnp.logical_not(first)),
                             jnp.logical_or(row < tm + HALO, jnp.logical_not(last)))
    hb = jnp.where(inside, h, 0.0).astype(BF16)

    def conv3(z, cols):
        w = dw_ref[:, cols]
        prev = pltpu.roll(z, 1, 0)[HALO:HALO + tm]
        nxt = pltpu.roll(z, te - 1, 0)[HALO:HALO + tm]
        return w[0:1] * prev + w[1:2] * z[HALO:HALO + tm] + w[2:3] * nxt + dwb_ref[:, cols]

    for c in range(D_FF // FF_CHUNK):
        cg = slice(c * FF_CHUNK, (c + 1) * FF_CHUNK)
        cv = slice(D_FF + c * FF_CHUNK, D_FF + (c + 1) * FF_CHUNK)
        g = conv3(_dot(hb, wup_ref[:, cg]), cg)
        v = conv3(_dot(hb, wup_ref[:, cv]), cv)
        a_scr[:, cg] = (g * jax.nn.sigmoid(g) * v).astype(BF16)
    y = _dot(a_scr[...], wdown_ref[...])
    out_ref[...] = x + mod_ref[0, 5:6, :] * y


def _ffn(x, mod3, lw, seq):
    t = x.shape[0]
    tm = TOKEN_TILE
    tps = seq // tm
    nh = tm // HALO
    last_halo = t // HALO - 1
    return pl.pallas_call(
        functools.partial(_ffn_kernel, tiles_per_seq=tps),
        grid=(t // tm,),
        in_specs=[
            pl.BlockSpec((HALO, D_MODEL), lambda i: (jnp.maximum(i * nh - 1, 0), 0)),
            pl.BlockSpec((tm, D_MODEL), lambda i: (i, 0)),
            pl.BlockSpec((HALO, D_MODEL), lambda i: (jnp.minimum((i + 1) * nh, last_halo), 0)),
            pl.BlockSpec((1, 6, D_MODEL), lambda i: (i // tps, 0, 0)),
            _const_spec((1, D_MODEL)),
            _const_spec((D_MODEL, 2 * D_FF)), _const_spec((FFN_CONV_WIDTH, 2 * D_FF)),
            _const_spec((1, 2 * D_FF)), _const_spec((D_FF, D_MODEL)),
        ],
        out_specs=pl.BlockSpec((tm, D_MODEL), lambda i: (i, 0)),
        out_shape=jax.ShapeDtypeStruct((t, D_MODEL), F32),
        scratch_shapes=[pltpu.VMEM((tm, D_FF), BF16)],
        compiler_params=_params(1),
        name="conv_ffn",
    )(x, x, x, mod3, lw["norm2"], lw["w_up"], lw["ffn_dw"], lw["ffn_dw_b"], lw["w_down"])


def _pad_heads(w, real):
    lead = w.shape[:-1]
    w = w.reshape(lead + (N_HEADS, real))
    w = jnp.pad(w, [(0, 0)] * len(lead) + [(0, 0), (0, LANES - real)])
    return w.reshape(lead + (N_HEADS * LANES,))


def _rope_tables(seq):
    half = QK_ROPE // 2
    inv_freq = jnp.power(ROPE_THETA, -jnp.arange(0, QK_ROPE, 2, dtype=F32) / QK_ROPE)
    ang = jnp.arange(seq, dtype=F32)[:, None] * inv_freq[None, :]
    cos, sin = jnp.cos(ang), jnp.sin(ang)
    z = lambda n: jnp.zeros((seq, n), F32)
    cf = jnp.concatenate([jnp.ones((seq, QK_NOPE), F32), cos, cos, z(LANES - QK_HEAD)], axis=1)
    sa = jnp.concatenate([z(QK_NOPE), -sin, z(half), z(LANES - QK_HEAD)], axis=1)
    sb = jnp.concatenate([z(QK_NOPE), z(half), sin, z(LANES - QK_HEAD)], axis=1)
    return cf, sa, sb


def _layer_weights(l, w_in, p):
    o_q, o_kv, o_kr, o_conv, o_sg, o_gate = 0, Q_LORA, Q_LORA + KV_LORA, Q_LORA + KV_LORA + QK_ROPE, \
        Q_LORA + KV_LORA + QK_ROPE + 2 * CONV_CH, Q_LORA + KV_LORA + QK_ROPE + 2 * CONV_CH + 2 * SG_CH
    wi = w_in[l]
    kr_block = jnp.pad(wi[:, o_kr:o_conv], ((0, 0), (QK_NOPE, LANES - QK_HEAD)))
    w_ukv = p["w_ukv"][l].reshape(KV_LORA, N_HEADS, QK_NOPE + V_HEAD)
    row = lambda a: a[l].reshape(1, -1)
    return {
        "norm1": row(p["norm1"]), "norm2": row(p["norm2"]),
        "w_a": jnp.concatenate([wi[:, o_q:o_kr], kr_block], axis=1).astype(BF16),
        "w_conv": wi[:, o_conv:o_sg].astype(BF16),
        "w_sg": wi[:, o_sg:o_gate].astype(BF16),
        "w_gate": wi[:, o_gate:].astype(BF16),
        "q_a_norm": row(p["q_a_norm"]), "kv_a_norm": row(p["kv_a_norm"]),
        "w_uq": _pad_heads(p["w_uq"][l], QK_HEAD).astype(BF16),
        "w_uk": _pad_heads(w_ukv[:, :, :QK_NOPE].reshape(KV_LORA, N_HEADS * QK_NOPE), QK_NOPE).astype(BF16),
        "w_uv": w_ukv[:, :, QK_NOPE:].reshape(KV_LORA, N_HEADS * V_HEAD).astype(BF16),
        "g_q": jnp.pad(p["q_head_norm"][l], (0, LANES - QK_HEAD)).reshape(1, LANES),
        "g_k": jnp.pad(p["k_head_norm"][l], (0, LANES - QK_HEAD)).reshape(1, LANES),
        "w_attn_o": p["w_attn_o"][l].astype(BF16),
        "conv_dw": p["conv_dw"][l], "conv_dw_b": row(p["conv_dw_b"]),
        "conv_ln_g": row(p["conv_ln_g"]), "conv_ln_b": row(p["conv_ln_b"]),
        "w_conv_o": p["w_conv_o"][l].astype(BF16),
        "sg_ln_g": row(p["sg_ln_g"]), "sg_ln_b": row(p["sg_ln_b"]),
        "sg_w": p["sg_w"][l].astype(BF16),
        "sg_bias": jnp.repeat(p["sg_b"][l].T, SG_CH // SG_GROUPS, axis=1),
        "w_sg_o": p["w_sg_o"][l].astype(BF16),
        "w_out": p["w_out"][l].astype(BF16),
        "w_up": p["w_up"][l].astype(BF16),
        "ffn_dw": p["ffn_dw"][l], "ffn_dw_b": row(p["ffn_dw_b"]),
        "w_down": p["w_down"][l].astype(BF16),
    }


def _encode(x, mod, layers, batch, seq):
    rope = _rope_tables(seq)
    x = x.reshape(batch * seq, D_MODEL)
    for l in range(DEPTH):
        lw = layers[l]
        mod3 = mod[l].reshape(batch, 6, D_MODEL)
        q, k, v, yc, ys, gate = _pre(x, mod3, lw, rope, seq)
        o = _attention(q, k, v, batch, seq)
        x = _merge(o, yc, ys, gate, x, mod3, lw, seq)
        x = _ffn(x, mod3, lw, seq)
    return x.reshape(batch, seq, D_MODEL)


def kernel(x_prompt, x_sample, c_prompt, c_sample, w_ada, b_ada, norm1, w_in, q_a_norm, w_uq, kv_a_norm, w_ukv,
           q_head_norm, k_head_norm, w_attn_o, conv_dw, conv_dw_b, conv_ln_g, conv_ln_b, w_conv_o,
           sg_ln_g, sg_ln_b, sg_w, sg_b, w_sg_o, w_out, norm2, w_up, ffn_dw, ffn_dw_b, w_down):
    p = dict(norm1=norm1, q_a_norm=q_a_norm, w_uq=w_uq, kv_a_norm=kv_a_norm, w_ukv=w_ukv,
             q_head_norm=q_head_norm, k_head_norm=k_head_norm, w_attn_o=w_attn_o, conv_dw=conv_dw,
             conv_dw_b=conv_dw_b, conv_ln_g=conv_ln_g, conv_ln_b=conv_ln_b, w_conv_o=w_conv_o,
             sg_ln_g=sg_ln_g, sg_ln_b=sg_ln_b, sg_w=sg_w, sg_b=sg_b, w_sg_o=w_sg_o, w_out=w_out,
             norm2=norm2, w_up=w_up, ffn_dw=ffn_dw, ffn_dw_b=ffn_dw_b, w_down=w_down)
    layers = [_layer_weights(l, w_in, p) for l in range(DEPTH)]

    bp, sp, _ = x_prompt.shape
    bs, ss, _ = x_sample.shape
    c_all = jnp.concatenate([c_prompt, c_sample, jnp.zeros((MOD_ROWS - bp - bs, D_MODEL), F32)], axis=0)
    mod = _modulation(c_all, w_ada, b_ada)
    y_prompt = _encode(x_prompt, mod[:, :bp], layers, bp, sp)
    y_sample = _encode(x_sample, mod[:, bp:bp + bs], layers, bs, ss)
    return (y_prompt, y_sample)
```
